```python
import math
import jax, jax.numpy as jnp
from jax import lax
import numpy as np

D_MODEL = 4096
BATCH = 4
SEQ = 2048
DEPTH = 1
DEC_BATCH = 128
DEC_SEQ = 4
PAST_LEN = 16384
PAGE_SIZE = 128

N_MEM = 256
CONV_CH = D_MODEL // 4
CONV_WIDTH = 31
CONV_STATE = CONV_WIDTH - 1
RET_HEADS = 8
RET_DK = D_MODEL // 16
RET_DV = D_MODEL // 16
RET_CHUNK = 128
MEM_HEADS = 4
MEM_DH = D_MODEL // 16
N_BRANCH = 3
PEER_HEADS = 8
PEER_NKEYS = 128
PEER_N = PEER_NKEYS * PEER_NKEYS
PEER_DQ = 256
PEER_TOPK = 16
PEER_BLOCK = 32
ROPE_BASE = 10000.0
EPS = 1e-6

IN_SIZES = (2 * CONV_CH, RET_HEADS * RET_DK, RET_HEADS * RET_DK, RET_HEADS * RET_DV,
            RET_HEADS * RET_DV, MEM_HEADS * MEM_DH, N_BRANCH * D_MODEL)
N_IN = sum(IN_SIZES)

kernel_name = 'gated_conv_retention_memory_peer_step'


def _split_cols(z):
    parts = []
    start = 0
    for n in IN_SIZES:
        parts.append(z[..., start:start + n])
        start += n
    return parts


def rmsnorm(x, g):
    xf = x.astype(jnp.float32)
    y = xf * lax.rsqrt(jnp.mean(xf * xf, axis=-1, keepdims=True) + EPS)
    return (y * g.astype(jnp.float32)).astype(x.dtype)


def layernorm(x, g, b):
    xf = x.astype(jnp.float32)
    mu = jnp.mean(xf, axis=-1, keepdims=True)
    xc = xf - mu
    y = xc * lax.rsqrt(jnp.mean(xc * xc, axis=-1, keepdims=True) + EPS)
    return (y * g.astype(jnp.float32) + b.astype(jnp.float32)).astype(x.dtype)


def rotary(x, pos):
    half = x.shape[-1] // 2
    inv = ROPE_BASE ** (-jnp.arange(half, dtype=jnp.float32) / half)
    ang = pos[:, None] * inv[None, :]
    cos = jnp.cos(ang)[None, :, None, :]
    sin = jnp.sin(ang)[None, :, None, :]
    xf = x.astype(jnp.float32)
    x1, x2 = xf[..., :half], xf[..., half:]
    return jnp.concatenate([x1 * cos - x2 * sin, x1 * sin + x2 * cos], axis=-1)


def ret_log_gamma():
    return jnp.log1p(-jnp.exp2(-5.0 - jnp.arange(RET_HEADS, dtype=jnp.float32)))


def retention(q, k, v, s0):
    B, T = q.shape[0], q.shape[1]
    C = math.gcd(T, RET_CHUNK)
    n = T // C
    lg = ret_log_gamma()
    idx = jnp.arange(C, dtype=jnp.float32)
    diff = idx[:, None] - idx[None, :]
    decay_in = jnp.where(diff[None] >= 0.0,
                         jnp.exp(jnp.maximum(diff, 0.0)[None] * lg[:, None, None]), 0.0)
    q_dec = jnp.exp((idx + 1.0)[:, None] * lg[None, :])
    k_dec = jnp.exp((C - 1.0 - idx)[:, None] * lg[None, :])
    c_dec = jnp.exp(C * lg)

    def blocks(a):
        return jnp.moveaxis(a.reshape((B, n, C) + a.shape[2:]), 1, 0)

    def step(s, inp):
        qc, kc, vc = inp
        att = jnp.einsum('bihd,bjhd->bhij', qc, kc) * decay_in[None]
        o = (jnp.einsum('bhij,bjhe->bihe', att, vc)
             + jnp.einsum('bihd,bhde->bihe', qc * q_dec[None, :, :, None], s))
        s = (s * c_dec[None, :, None, None]
             + jnp.einsum('bjhd,bjhe->bhde', kc * k_dec[None, :, :, None], vc))
        return s, o

    s_fin, o = lax.scan(step, s0, (blocks(q), blocks(k), blocks(v)))
    o = jnp.moveaxis(o, 0, 1).reshape(B, T, RET_HEADS, RET_DV)
    return o, s_fin


def mixer_layer(x, pos0, conv_st, ret_st, mem_k, mem_v, norm_mix, w_in, conv_w, conv_b,
                conv_ln_g, conv_ln_b, w_conv_out, ret_gn_g, w_ret_out, w_mem_out, w_out):
    B, T, _ = x.shape
    h = rmsnorm(x, norm_mix)
    z = h @ w_in
    zc, zq, zk, zv, zg, zm, zgate = _split_cols(z)

    a, b = jnp.split(zc, 2, axis=-1)
    u = a * jax.nn.sigmoid(b)
    full = jnp.concatenate([conv_st.astype(u.dtype), u], axis=1)
    new_conv = full[:, -CONV_STATE:]
    c = lax.conv_general_dilated(full, conv_w[:, None, :].astype(full.dtype), (1,), 'VALID',
                                 dimension_numbers=('NWC', 'WIO', 'NWC'),
                                 feature_group_count=CONV_CH) + conv_b
    c = jax.nn.silu(layernorm(c, conv_ln_g, conv_ln_b))
    out_a = c @ w_conv_out

    pos = jnp.arange(T, dtype=jnp.float32) + float(pos0)
    q = rotary(zq.reshape(B, T, RET_HEADS, RET_DK), pos) * (RET_DK ** -0.5)
    k = rotary(zk.reshape(B, T, RET_HEADS, RET_DK), pos)
    v = zv.reshape(B, T, RET_HEADS, RET_DV).astype(jnp.float32)
    o, new_ret = retention(q, k, v, ret_st.astype(jnp.float32))
    mu = jnp.mean(o, axis=-1, keepdims=True)
    oc = o - mu
    o = oc * lax.rsqrt(jnp.mean(oc * oc, axis=-1, keepdims=True) + EPS)
    o = (o.reshape(B, T, RET_HEADS * RET_DV) * ret_gn_g.astype(jnp.float32)).astype(x.dtype)
    out_b = (o * jax.nn.silu(zg)) @ w_ret_out

    qm = zm.reshape(B, T, MEM_HEADS, MEM_DH).astype(jnp.float32)
    sc = jnp.einsum('bthd,bmhd->bhtm', qm, mem_k.astype(jnp.float32)) * (MEM_DH ** -0.5)
    p = jax.nn.softmax(sc, axis=-1)
    om = jnp.einsum('bhtm,bmhd->bthd', p, mem_v.astype(jnp.float32))
    out_c = om.reshape(B, T, MEM_HEADS * MEM_DH).astype(x.dtype) @ w_mem_out

    g = jax.nn.sigmoid(zgate.reshape(B, T, N_BRANCH, D_MODEL))
    merged = g[:, :, 0] * out_a + g[:, :, 1] * out_b + g[:, :, 2] * out_c
    return x + merged @ w_out, new_conv, new_ret.astype(ret_st.dtype)


def peer(h, w_q, subkeys, u_tab, v_tab):
    shape = h.shape
    hf = h.reshape(-1, D_MODEL)
    n_tok = hf.shape[0]
    q = (hf @ w_q).reshape(n_tok, PEER_HEADS, 2, PEER_DQ // 2)
    s = jnp.einsum('nphd,phkd->nphk', q, subkeys.astype(q.dtype)).astype(jnp.float32)
    sv, si = lax.top_k(s, PEER_TOPK)
    cand = (sv[:, :, 0, :, None] + sv[:, :, 1, None, :]).reshape(n_tok, PEER_HEADS, PEER_TOPK * PEER_TOPK)
    cidx = (si[:, :, 0, :, None] * PEER_NKEYS + si[:, :, 1, None, :]).reshape(n_tok, PEER_HEADS, PEER_TOPK * PEER_TOPK)
    tv, tp = lax.top_k(cand, PEER_TOPK)
    eidx = jnp.take_along_axis(cidx, tp, axis=-1)
    gate = jax.nn.softmax(tv, axis=-1).astype(h.dtype)
    pad = (-n_tok) % PEER_BLOCK
    nb = (n_tok + pad) // PEER_BLOCK
    hp = jnp.pad(hf, ((0, pad), (0, 0))).reshape(nb, PEER_BLOCK, D_MODEL)
    ep = jnp.pad(eidx, ((0, pad), (0, 0), (0, 0))).reshape(nb, PEER_BLOCK, PEER_HEADS, PEER_TOPK)
    gp = jnp.pad(gate, ((0, pad), (0, 0), (0, 0))).reshape(nb, PEER_BLOCK, PEER_HEADS, PEER_TOPK)

    def blk(args):
        hb, eb, gb = args
        ub = jnp.take(u_tab, eb, axis=0)
        act = jax.nn.gelu(jnp.einsum('nd,npkd->npk', hb, ub))
        vb = jnp.take(v_tab, eb, axis=0)
        return jnp.einsum('npk,npkd->nd', gb * act, vb)

    out = lax.map(blk, (hp, ep, gp))
    return out.reshape(-1, D_MODEL)[:n_tok].reshape(shape)


def setup_inputs(seed: int = 0) -> dict:
    key = jax.random.key(seed)
    ks = jax.random.split(key, 32)
    L = DEPTH

    def nrm(k, shape, s):
        return jax.random.normal(k, shape, jnp.float32) * s

    return {
        'x_prompt': nrm(ks[0], (BATCH, SEQ, D_MODEL), 1.0),
        'x_sample': nrm(ks[1], (DEC_BATCH, DEC_SEQ, D_MODEL), 1.0),
        'mem_prompt': nrm(ks[2], (BATCH, N_MEM, D_MODEL), 1.0),
        'state_conv': nrm(ks[3], (L, DEC_BATCH, CONV_STATE, CONV_CH), 0.5),
        'state_ret': nrm(ks[4], (L, DEC_BATCH, RET_HEADS, RET_DK, RET_DV), 8.0),
        'cache_mem_k': nrm(ks[5], (L, DEC_BATCH, N_MEM, MEM_HEADS, MEM_DH), 1.0),
        'cache_mem_v': nrm(ks[6], (L, DEC_BATCH, N_MEM, MEM_HEADS, MEM_DH), 1.0),
        'norm_mix': 1.0 + nrm(ks[7], (L, D_MODEL), 0.02),
        'norm_mem': 1.0 + nrm(ks[8], (L, D_MODEL), 0.02),
        'w_in': nrm(ks[9], (L, D_MODEL, N_IN), D_MODEL ** -0.5),
        'conv_w': nrm(ks[10], (L, CONV_WIDTH, CONV_CH), CONV_WIDTH ** -0.5),
        'conv_b': nrm(ks[11], (L, CONV_CH), 0.02),
        'conv_ln_g': 1.0 + nrm(ks[12], (L, CONV_CH), 0.02),
        'conv_ln_b': nrm(ks[13], (L, CONV_CH), 0.02),
        'w_conv_out': nrm(ks[14], (L, CONV_CH, D_MODEL), CONV_CH ** -0.5),
        'ret_gn_g': 1.0 + nrm(ks[15], (L, RET_HEADS * RET_DV), 0.02),
        'w_ret_out': nrm(ks[16], (L, RET_HEADS * RET_DV, D_MODEL), (RET_HEADS * RET_DV) ** -0.5),
        'w_mem_k': nrm(ks[17], (L, D_MODEL, MEM_HEADS * MEM_DH), D_MODEL ** -0.5),
        'w_mem_v': nrm(ks[18], (L, D_MODEL, MEM_HEADS * MEM_DH), D_MODEL ** -0.5),
        'w_mem_out': nrm(ks[19], (L, MEM_HEADS * MEM_DH, D_MODEL), (MEM_HEADS * MEM_DH) ** -0.5),
        'w_out': nrm(ks[20], (L, D_MODEL, D_MODEL), D_MODEL ** -0.5),
        'norm_ffn': 1.0 + nrm(ks[21], (L, D_MODEL), 0.02),
        'w_peer_q': nrm(ks[22], (L, D_MODEL, PEER_HEADS * PEER_DQ), D_MODEL ** -0.5),
        'peer_subkeys': nrm(ks[23], (L, PEER_HEADS, 2, PEER_NKEYS, PEER_DQ // 2), (PEER_DQ // 2) ** -0.5),
        'peer_u': nrm(ks[24], (L, PEER_N, D_MODEL), D_MODEL ** -0.5),
        'peer_v': nrm(ks[25], (L, PEER_N, D_MODEL), PEER_HEADS ** -0.5),
        'norm_final': 1.0 + nrm(ks[26], (D_MODEL,), 0.02),
    }


def reference(x_prompt, x_sample, mem_prompt, state_conv, state_ret, cache_mem_k, cache_mem_v,
              norm_mix, norm_mem, w_in, conv_w, conv_b, conv_ln_g, conv_ln_b, w_conv_out,
              ret_gn_g, w_ret_out, w_mem_k, w_mem_v, w_mem_out, w_out, norm_ffn,
              w_peer_q, peer_subkeys, peer_u, peer_v, norm_final):
    xp, xs = x_prompt, x_sample
    bp = x_prompt.shape[0]
    conv_p, ret_p, memk_p, memv_p, conv_s, ret_s = [], [], [], [], [], []
    for l in range(DEPTH):
        mh = rmsnorm(mem_prompt, norm_mem[l])
        mk = (mh @ w_mem_k[l]).reshape(bp, N_MEM, MEM_HEADS, MEM_DH)
        mv = (mh @ w_mem_v[l]).reshape(bp, N_MEM, MEM_HEADS, MEM_DH)
        conv0 = jnp.zeros((bp, CONV_STATE, CONV_CH), xp.dtype)
        ret0 = jnp.zeros((bp, RET_HEADS, RET_DK, RET_DV), state_ret.dtype)
        lw = (norm_mix[l], w_in[l], conv_w[l], conv_b[l], conv_ln_g[l], conv_ln_b[l], w_conv_out[l],
              ret_gn_g[l], w_ret_out[l], w_mem_out[l], w_out[l])
        xp, cp, rp = mixer_layer(xp, 0, conv0, ret0, mk, mv, *lw)
        xs, cs, rs = mixer_layer(xs, PAST_LEN, state_conv[l], state_ret[l],
                                 cache_mem_k[l], cache_mem_v[l], *lw)
        xp = xp + peer(rmsnorm(xp, norm_ffn[l]), w_peer_q[l], peer_subkeys[l], peer_u[l], peer_v[l])
        xs = xs + peer(rmsnorm(xs, norm_ffn[l]), w_peer_q[l], peer_subkeys[l], peer_u[l], peer_v[l])
        conv_p.append(cp)
        ret_p.append(rp)
        memk_p.append(mk)
        memv_p.append(mv)
        conv_s.append(cs)
        ret_s.append(rs)
    y_prompt = rmsnorm(xp, norm_final)
    y_sample = rmsnorm(xs, norm_final)
    return (y_prompt, y_sample, jnp.stack(conv_p), jnp.stack(ret_p), jnp.stack(memk_p),
            jnp.stack(memv_p), jnp.stack(conv_s), jnp.stack(ret_s))
```

```python
import functools
import math

import jax
import jax.numpy as jnp
from jax import lax
from jax.experimental import pallas as pl
from jax.experimental.pallas import tpu as pltpu

F32 = jnp.float32
BF16 = jnp.bfloat16

EPS = 1e-6
CONV_WIDTH = 31
CONV_STATE = CONV_WIDTH - 1
RET_HEADS = 8
RET_CHUNK = 128
MEM_HEADS = 4
N_BRANCH = 3
PEER_HEADS = 8
PEER_NKEYS = 128
PEER_TOPK = 16
ROPE_BASE = 10000.0
PAST_LEN = 16384

LANES = 128
VMEM_LIMIT = 56 * 1024 * 1024


def _params(sem):
    return pltpu.CompilerParams(dimension_semantics=sem, vmem_limit_bytes=VMEM_LIMIT)


def _pick(n, prefs):
    for p in prefs:
        if n % p == 0:
            return p
    raise ValueError(f"no tile in {prefs} divides {n}")


def _dot(a, b):
    return jnp.dot(a, b, preferred_element_type=F32)


def _dot_nt(a, b):
    return lax.dot_general(a, b, (((1,), (1,)), ((), ())), preferred_element_type=F32)


def _dot_tn(a, b):
    return lax.dot_general(a, b, (((0,), (0,)), ((), ())), preferred_element_type=F32)


def _rmsnorm_body(x_ref, g_ref, o_ref):
    x = x_ref[...]
    ms = jnp.mean(x * x, axis=-1, keepdims=True)
    o_ref[...] = (x * lax.rsqrt(ms + EPS) * g_ref[...]).astype(o_ref.dtype)


def _rmsnorm(x, g, out_dtype):
    n, d = x.shape
    tm = _pick(n, (256, 128, 64, 32, 16, 8))
    return pl.pallas_call(
        _rmsnorm_body,
        grid=(n // tm,),
        in_specs=[pl.BlockSpec((tm, d), lambda i: (i, 0)),
                  pl.BlockSpec((1, d), lambda i: (0, 0))],
        out_specs=pl.BlockSpec((tm, d), lambda i: (i, 0)),
        out_shape=jax.ShapeDtypeStruct((n, d), out_dtype),
        compiler_params=_params(("parallel",)),
        name="rmsnorm",
    )(x, g.reshape(1, d))


def _add_rmsnorm_body(x_ref, y_ref, g_ref, o_ref):
    x = x_ref[...] + y_ref[...]
    ms = jnp.mean(x * x, axis=-1, keepdims=True)
    o_ref[...] = (x * lax.rsqrt(ms + EPS) * g_ref[...]).astype(o_ref.dtype)


def _add_rmsnorm(x, y, g):
    n, d = x.shape
    tm = _pick(n, (256, 128, 64, 32, 16, 8))
    return pl.pallas_call(
        _add_rmsnorm_body,
        grid=(n // tm,),
        in_specs=[pl.BlockSpec((tm, d), lambda i: (i, 0)),
                  pl.BlockSpec((tm, d), lambda i: (i, 0)),
                  pl.BlockSpec((1, d), lambda i: (0, 0))],
        out_specs=pl.BlockSpec((tm, d), lambda i: (i, 0)),
        out_shape=jax.ShapeDtypeStruct((n, d), F32),
        compiler_params=_params(("parallel",)),
        name="add_rmsnorm",
    )(x, y, g.reshape(1, d))


def _mm_body(a_ref, w_ref, o_ref):
    o_ref[...] = _dot(a_ref[...], w_ref[...]).astype(o_ref.dtype)


def _mm_res_body(a_ref, w_ref, r_ref, o_ref):
    o_ref[...] = r_ref[...] + _dot(a_ref[...], w_ref[...])


def _matmul(a, w, residual=None, out_dtype=F32):
    m, k = a.shape
    n = w.shape[1]
    tm = _pick(m, (1088, 1024, 512, 256, 128))
    tn = _pick(n, (1024, 512, 256, 128) if residual is None else (512, 256, 128))
    in_specs = [pl.BlockSpec((tm, k), lambda i, j: (i, 0)),
                pl.BlockSpec((k, tn), lambda i, j: (0, j))]
    args = [a, w]
    body = _mm_body
    if residual is not None:
        in_specs.append(pl.BlockSpec((tm, tn), lambda i, j: (i, j)))
        args.append(residual)
        body = _mm_res_body
    return pl.pallas_call(
        body,
        grid=(m // tm, n // tn),
        in_specs=in_specs,
        out_specs=pl.BlockSpec((tm, tn), lambda i, j: (i, j)),
        out_shape=jax.ShapeDtypeStruct((m, n), out_dtype),
        compiler_params=_params(("parallel", "parallel")),
        name="matmul",
    )(*args)


def _conv_tail(c, cb_ref, lg_ref, lb_ref):
    c = c + cb_ref[...]
    mu = jnp.mean(c, axis=-1, keepdims=True)
    cc = c - mu
    y = cc * lax.rsqrt(jnp.mean(cc * cc, axis=-1, keepdims=True) + EPS)
    y = y * lg_ref[...] + lb_ref[...]
    return y * jax.nn.sigmoid(y)


def _conv_prompt_body(za_ref, zb_ref, cw_ref, cb_ref, lg_ref, lb_ref, o_ref, st_ref, full_ref, *, tt):
    t = pl.program_id(1)
    halo0 = 32 - CONV_STATE

    @pl.when(t == 0)
    def _():
        full_ref[0:32, :] = jnp.zeros((32, full_ref.shape[1]), F32)

    u = za_ref[...] * jax.nn.sigmoid(zb_ref[...])
    full_ref[32:32 + tt, :] = u
    c = jnp.zeros_like(u)
    for w in range(CONV_WIDTH):
        c = c + full_ref[halo0 + w:halo0 + w + tt, :] * cw_ref[w:w + 1, :]
    o_ref[...] = _conv_tail(c, cb_ref, lg_ref, lb_ref).astype(o_ref.dtype)
    tail = full_ref[tt + halo0:tt + 32, :]
    full_ref[halo0:32, :] = tail
    st_ref[0] = tail


def _conv_prompt(z, nb, t_len, cc, conv_w, conv_b, ln_g, ln_b):
    tt = _pick(t_len, (256, 128))
    nt = t_len // tt
    vec = lambda: pl.BlockSpec((1, cc), lambda b, t: (0, 0))
    return pl.pallas_call(
        functools.partial(_conv_prompt_body, tt=tt),
        grid=(nb, nt),
        in_specs=[pl.BlockSpec((tt, cc), lambda b, t: (b * nt + t, 0)),
                  pl.BlockSpec((tt, cc), lambda b, t: (b * nt + t, 1)),
                  pl.BlockSpec((CONV_WIDTH, cc), lambda b, t: (0, 0)),
                  vec(), vec(), vec()],
        out_specs=[pl.BlockSpec((tt, cc), lambda b, t: (b * nt + t, 0)),
                   pl.BlockSpec((1, CONV_STATE, cc), lambda b, t: (b, 0, 0))],
        out_shape=[jax.ShapeDtypeStruct((nb * t_len, cc), BF16),
                   jax.ShapeDtypeStruct((nb, CONV_STATE, cc), F32)],
        scratch_shapes=[pltpu.VMEM((32 + tt, cc), F32)],
        compiler_params=_params(("parallel", "arbitrary")),
        name="conv_prompt",
    )(z, z, conv_w, conv_b.reshape(1, cc), ln_g.reshape(1, cc), ln_b.reshape(1, cc))


def _conv_sample_body(za_ref, zb_ref, st_ref, cw_ref, cb_ref, lg_ref, lb_ref, o_ref, ns_ref, full_ref, *, bb, dt):
    for i in range(bb):
        u = za_ref[i] * jax.nn.sigmoid(zb_ref[i])
        full_ref[0:CONV_STATE, :] = st_ref[i]
        full_ref[CONV_STATE:CONV_STATE + dt, :] = u
        c = jnp.zeros_like(u)
        for w in range(CONV_WIDTH):
            c = c + full_ref[w:w + dt, :] * cw_ref[w:w + 1, :]
        o_ref[i] = _conv_tail(c, cb_ref, lg_ref, lb_ref).astype(o_ref.dtype)
        ns_ref[i] = full_ref[dt:dt + CONV_STATE, :]


def _conv_sample(zs, state, cc, conv_w, conv_b, ln_g, ln_b):
    db, dt, _ = zs.shape
    bb = _pick(db, (8, 4, 2, 1))
    vec = lambda: pl.BlockSpec((1, cc), lambda b: (0, 0))
    return pl.pallas_call(
        functools.partial(_conv_sample_body, bb=bb, dt=dt),
        grid=(db // bb,),
        in_specs=[pl.BlockSpec((bb, dt, cc), lambda b: (b, 0, 0)),
                  pl.BlockSpec((bb, dt, cc), lambda b: (b, 0, 1)),
                  pl.BlockSpec((bb, CONV_STATE, cc), lambda b: (b, 0, 0)),
                  pl.BlockSpec((CONV_WIDTH, cc), lambda b: (0, 0)),
                  vec(), vec(), vec()],
        out_specs=[pl.BlockSpec((bb, dt, cc), lambda b: (b, 0, 0)),
                   pl.BlockSpec((bb, CONV_STATE, cc), lambda b: (b, 0, 0))],
        out_shape=[jax.ShapeDtypeStruct((db, dt, cc), BF16),
                   jax.ShapeDtypeStruct((db, CONV_STATE, cc), F32)],
        scratch_shapes=[pltpu.VMEM((CONV_STATE + dt + 6, cc), F32)],
        compiler_params=_params(("parallel",)),
        name="conv_sample",
    )(zs, zs, state, conv_w, conv_b.reshape(1, cc), ln_g.reshape(1, cc), ln_b.reshape(1, cc))


def _ret_log_gamma():
    return jnp.log1p(-jnp.exp2(-5.0 - jnp.arange(RET_HEADS, dtype=F32)))


def _ret_tables(c):
    lg = _ret_log_gamma()
    idx = jnp.arange(c, dtype=F32)
    diff = idx[:, None] - idx[None, :]
    decay_in = jnp.where(diff[None] >= 0.0,
                         jnp.exp(jnp.maximum(diff, 0.0)[None] * lg[:, None, None]), 0.0)
    q_dec = jnp.exp((idx + 1.0)[:, None] * lg[None, :])
    k_dec = jnp.exp((c - 1.0 - idx)[:, None] * lg[None, :])
    c_dec = jnp.exp(c * lg)
    return decay_in, q_dec.T[:, :, None], k_dec.T[:, :, None], c_dec


def _rope_tables(t_len, pos0, half):
    pos = jnp.arange(t_len, dtype=F32) + float(pos0)
    inv = ROPE_BASE ** (-jnp.arange(half, dtype=F32) / half)
    ang = pos[:, None] * inv[None, :]
    return jnp.cos(ang), jnp.sin(ang)


def _rope(x, cos, sin, half):
    x1, x2 = x[:, :half], x[:, half:]
    return jnp.concatenate([x1 * cos - x2 * sin, x1 * sin + x2 * cos], axis=-1)


def _ret_step(q, k, v, g, s, cos, sin, dec, qd, kd, cdec, gn, dk):
    half = dk // 2
    qr = _rope(q, cos, sin, half) * (dk ** -0.5)
    kr = _rope(k, cos, sin, half)
    vb = v.astype(BF16)
    att = _dot_nt(qr.astype(BF16), kr.astype(BF16)) * dec
    o = _dot(att.astype(BF16), vb) + _dot((qr * qd).astype(BF16), s.astype(BF16))
    s_new = s * cdec + _dot_tn((kr * kd).astype(BF16), vb)
    mu = jnp.mean(o, axis=-1, keepdims=True)
    oc = o - mu
    on = oc * lax.rsqrt(jnp.mean(oc * oc, axis=-1, keepdims=True) + EPS) * gn
    return on * (g * jax.nn.sigmoid(g)), s_new


def _ret_prompt_body(cdec_ref, q_ref, k_ref, v_ref, g_ref, cos_ref, sin_ref, dec_ref, qd_ref, kd_ref, gn_ref,
                     o_ref, st_ref, s_ref, *, dk):
    h = pl.program_id(1)
    c = pl.program_id(2)

    @pl.when(c == 0)
    def _():
        s_ref[...] = jnp.zeros_like(s_ref)

    out, s_new = _ret_step(q_ref[...], k_ref[...], v_ref[...], g_ref[...], s_ref[...], cos_ref[...], sin_ref[...],
                           dec_ref[0], qd_ref[0], kd_ref[0], cdec_ref[h], gn_ref[...], dk)
    o_ref[...] = out.astype(o_ref.dtype)
    s_ref[...] = s_new
    st_ref[0, 0] = s_new


def _ret_prompt(z, nb, t_len, dk, off_q, gn_g):
    ch = math.gcd(t_len, RET_CHUNK)
    nc = t_len // ch
    nh = RET_HEADS
    decay_in, qd, kd, cdec = _ret_tables(ch)
    cos, sin = _rope_tables(t_len, 0, dk // 2)
    oq = off_q // dk
    zspec = lambda sec: pl.BlockSpec((ch, dk), lambda b, h, c: (b * nc + c, oq + sec * nh + h))
    return pl.pallas_call(
        functools.partial(_ret_prompt_body, dk=dk),
        grid=(nb, nh, nc),
        in_specs=[pl.BlockSpec(memory_space=pltpu.SMEM),
                  zspec(0), zspec(1), zspec(2), zspec(3),
                  pl.BlockSpec((ch, dk // 2), lambda b, h, c: (c, 0)),
                  pl.BlockSpec((ch, dk // 2), lambda b, h, c: (c, 0)),
                  pl.BlockSpec((1, ch, ch), lambda b, h, c: (h, 0, 0)),
                  pl.BlockSpec((1, ch, 1), lambda b, h, c: (h, 0, 0)),
                  pl.BlockSpec((1, ch, 1), lambda b, h, c: (h, 0, 0)),
                  pl.BlockSpec((1, dk), lambda b, h, c: (0, h))],
        out_specs=[pl.BlockSpec((ch, dk), lambda b, h, c: (b * nc + c, h)),
                   pl.BlockSpec((1, 1, dk, dk), lambda b, h, c: (b, h, 0, 0))],
        out_shape=[jax.ShapeDtypeStruct((nb * t_len, nh * dk), BF16),
                   jax.ShapeDtypeStruct((nb, nh, dk, dk), F32)],
        scratch_shapes=[pltpu.VMEM((dk, dk), F32)],
        compiler_params=_params(("parallel", "parallel", "arbitrary")),
        name="ret_prompt",
    )(cdec, z, z, z, z, cos, sin, decay_in, qd, kd, gn_g.reshape(1, nh * dk))


def _ret_sample_body(cdec_ref, q_ref, k_ref, v_ref, g_ref, s_ref, cos_ref, sin_ref, dec_ref, qd_ref, kd_ref, gn_ref,
                     o_ref, st_ref, *, dk):
    cos = cos_ref[...]
    sin = sin_ref[...]
    for h in range(RET_HEADS):
        sl = slice(h * dk, (h + 1) * dk)
        out, s_new = _ret_step(q_ref[0, :, sl], k_ref[0, :, sl], v_ref[0, :, sl], g_ref[0, :, sl], s_ref[0, h],
                               cos, sin, dec_ref[h], qd_ref[h], kd_ref[h], cdec_ref[h], gn_ref[:, sl], dk)
        o_ref[0, :, sl] = out.astype(o_ref.dtype)
        st_ref[0, h] = s_new


def _ret_sample(zs, state, dk, off_q, gn_g):
    db, dt, _ = zs.shape
    nh = RET_HEADS
    ch = math.gcd(dt, RET_CHUNK)
    assert ch == dt, "sample sequences longer than one retention chunk are not supported"
    decay_in, qd, kd, cdec = _ret_tables(ch)
    cos, sin = _rope_tables(dt, PAST_LEN, dk // 2)
    w = nh * dk
    oq = off_q // w
    zspec = lambda sec: pl.BlockSpec((1, dt, w), lambda b: (b, 0, oq + sec))
    full = lambda shp: pl.BlockSpec(shp, lambda b: (0,) * len(shp))
    return pl.pallas_call(
        functools.partial(_ret_sample_body, dk=dk),
        grid=(db,),
        in_specs=[pl.BlockSpec(memory_space=pltpu.SMEM),
                  zspec(0), zspec(1), zspec(2), zspec(3),
                  pl.BlockSpec((1, nh, dk, dk), lambda b: (b, 0, 0, 0)),
                  full((dt, dk // 2)), full((dt, dk // 2)),
                  full((nh, ch, ch)), full((nh, ch, 1)), full((nh, ch, 1)), full((1, w))],
        out_specs=[pl.BlockSpec((1, dt, w), lambda b: (b, 0, 0)),
                   pl.BlockSpec((1, nh, dk, dk), lambda b: (b, 0, 0, 0))],
        out_shape=[jax.ShapeDtypeStruct((db, dt, w), BF16),
                   jax.ShapeDtypeStruct((db, nh, dk, dk), F32)],
        compiler_params=_params(("parallel",)),
        name="ret_sample",
    )(cdec, zs, zs, zs, zs, state, cos, sin, decay_in, qd, kd, gn_g.reshape(1, w))


def _attn_heads(q, k_all, v_all, dh):
    outs = []
    for h in range(MEM_HEADS):
        sl = slice(h * dh, (h + 1) * dh)
        sc = _dot_nt(q[:, sl].astype(BF16), k_all[:, sl].astype(BF16)) * (dh ** -0.5)
        e = jnp.exp(sc - jnp.max(sc, axis=-1, keepdims=True))
        p = e / jnp.sum(e, axis=-1, keepdims=True)
        outs.append(_dot(p.astype(BF16), v_all[:, sl].astype(BF16)))
    return jnp.concatenate(outs, axis=-1)


def _attn_prompt_body(q_ref, k_ref, v_ref, o_ref, *, dh):
    o_ref[...] = _attn_heads(q_ref[...], k_ref[...], v_ref[...], dh).astype(o_ref.dtype)


def _attn_prompt(z, mk, mv, nb, t_len, n_mem, dh, off_m):
    w = MEM_HEADS * dh
    tq = _pick(t_len, (512, 256, 128))
    nt = t_len // tq
    om = off_m // w
    return pl.pallas_call(
        functools.partial(_attn_prompt_body, dh=dh),
        grid=(nb, nt),
        in_specs=[pl.BlockSpec((tq, w), lambda b, t: (b * nt + t, om)),
                  pl.BlockSpec((n_mem, w), lambda b, t: (b, 0)),
                  pl.BlockSpec((n_mem, w), lambda b, t: (b, 0))],
        out_specs=pl.BlockSpec((tq, w), lambda b, t: (b * nt + t, 0)),
        out_shape=jax.ShapeDtypeStruct((nb * t_len, w), BF16),
        compiler_params=_params(("parallel", "parallel")),
        name="attn_prompt",
    )(z, mk, mv)


def _attn_sample_body(q_ref, k_ref, v_ref, o_ref, *, dh):
    o_ref[0] = _attn_heads(q_ref[0], k_ref[0], v_ref[0], dh).astype(o_ref.dtype)


def _attn_sample(zs, ck, cv, dh, off_m):
    db, dt, _ = zs.shape
    n_mem = ck.shape[1]
    w = MEM_HEADS * dh
    om = off_m // w
    return pl.pallas_call(
        functools.partial(_attn_sample_body, dh=dh),
        grid=(db,),
        in_specs=[pl.BlockSpec((1, dt, w), lambda b: (b, 0, om)),
                  pl.BlockSpec((1, n_mem, w), lambda b: (b, 0, 0)),
                  pl.BlockSpec((1, n_mem, w), lambda b: (b, 0, 0))],
        out_specs=pl.BlockSpec((1, dt, w), lambda b: (b, 0, 0)),
        out_shape=jax.ShapeDtypeStruct((db, dt, w), BF16),
        compiler_params=_params(("parallel",)),
        name="attn_sample",
    )(zs, ck, cv)


def _merge_body(ca_ref, rb_ref, mc_ref, wc_ref, wr_ref, wm_ref, g0_ref, g1_ref, g2_ref, o_ref):
    m = jax.nn.sigmoid(g0_ref[...]) * _dot(ca_ref[...], wc_ref[...])
    m = m + jax.nn.sigmoid(g1_ref[...]) * _dot(rb_ref[...], wr_ref[...])
    m = m + jax.nn.sigmoid(g2_ref[...]) * _dot(mc_ref[...], wm_ref[...])
    o_ref[...] = m.astype(o_ref.dtype)


def _merge(ca, rb, mc, wc, wr, wm, z, off_gate):
    n = ca.shape[0]
    d = wc.shape[1]
    tm = _pick(n, (1088, 1024, 512, 256, 128))
    tn = 512
    og = off_gate // tn
    nd = d // tn
    act = lambda a: pl.BlockSpec((tm, a.shape[1]), lambda i, j: (i, 0))
    wgt = lambda a: pl.BlockSpec((a.shape[0], tn), lambda i, j: (0, j))
    gate = lambda b: pl.BlockSpec((tm, tn), lambda i, j: (i, og + b * nd + j))
    return pl.pallas_call(
        _merge_body,
        grid=(n // tm, nd),
        in_specs=[act(ca), act(rb), act(mc), wgt(wc), wgt(wr), wgt(wm), gate(0), gate(1), gate(2)],
        out_specs=pl.BlockSpec((tm, tn), lambda i, j: (i, j)),
        out_shape=jax.ShapeDtypeStruct((n, d), BF16),
        compiler_params=_params(("parallel", "parallel")),
        name="merge",
    )(ca, rb, mc, wc, wr, wm, z, z, z)


def _top16(x, iota, n):
    work = x
    rank = jnp.full(x.shape, float(PEER_TOPK), F32)
    vals, idxs = [], []
    for r in range(PEER_TOPK):
        m = jnp.max(work, axis=0, keepdims=True)
        idx = jnp.min(jnp.where(work == m, iota, float(n)), axis=0, keepdims=True)
        hit = iota == idx
        rank = jnp.where(hit, float(r), rank)
        work = jnp.where(hit, -jnp.inf, work)
        vals.append(m)
        idxs.append(idx)
    return jnp.concatenate(vals, axis=0), jnp.concatenate(idxs, axis=0), rank


def _route_body(q_ref, sk_ref, r1_ref, c0_ref, f_ref, e1_ref):
    nk = PEER_NKEYS
    tm = q_ref.shape[0]
    iota = lax.broadcasted_iota(jnp.int32, (nk, tm), 0).astype(F32)
    iota2 = lax.broadcasted_iota(jnp.int32, (PEER_TOPK * PEER_TOPK, tm), 0).astype(F32)
    s0 = _dot_nt(sk_ref[0, 0], q_ref[:, :nk].astype(BF16))
    s1 = _dot_nt(sk_ref[0, 1], q_ref[:, nk:].astype(BF16))
    sv0, _, rank0 = _top16(s0, iota, nk)
    sv1, _, rank1 = _top16(s1, iota, nk)
    cand = jnp.concatenate([sv0[i:i + 1] + sv1 for i in range(PEER_TOPK)], axis=0)
    tv, tp, _ = _top16(cand, iota2, PEER_TOPK * PEER_TOPK)
    irow = jnp.floor(tp * (1.0 / PEER_TOPK))
    c0 = jnp.zeros((nk, tm), F32)
    for i in range(PEER_TOPK):
        cnt = jnp.sum(jnp.where(irow == float(i), 1.0, 0.0), axis=0, keepdims=True)
        c0 = jnp.where(rank0 == float(i), cnt, c0)
    z = jnp.sum(jnp.exp(tv - tv[0:1]), axis=0, keepdims=True)
    r1_ref[0] = rank1
    c0_ref[0] = c0
    f_ref[0] = jnp.exp(s0 - sv0[0:1]) / z
    e1_ref[0] = jnp.exp(s1 - sv1[0:1])


def _route(q, subkeys):
    n = q.shape[0]
    nk = PEER_NKEYS
    tm = _pick(n, (256, 128))
    spec = pl.BlockSpec((1, nk, tm), lambda i, p: (p, 0, i))
    shp = jax.ShapeDtypeStruct((PEER_HEADS, nk, n), F32)
    return pl.pallas_call(
        _route_body,
        grid=(n // tm, PEER_HEADS),
        in_specs=[pl.BlockSpec((tm, 2 * nk), lambda i, p: (i, p)),
                  pl.BlockSpec((1, 2, nk, nk), lambda i, p: (p, 0, 0, 0))],
        out_specs=[spec, spec, spec, spec],
        out_shape=[shp, shp, shp, shp],
        compiler_params=_params(("parallel", "parallel")),
        name="peer_route",
    )(q, subkeys)


def _peer_body(h_ref, u_ref, v_ref, r1_ref, c0_ref, f_ref, e1_ref, o_ref, *, ac):
    c = pl.program_id(1)
    nk = PEER_NKEYS

    @pl.when(c == 0)
    def _():
        o_ref[...] = jnp.zeros_like(o_ref)

    act = jax.nn.gelu(_dot_nt(u_ref[...], h_ref[...]))
    parts = []
    for al in range(ac):
        a = c * ac + al
        w = None
        for p in range(PEER_HEADS):
            cnt = c0_ref[p, pl.ds(a, 1), :]
            fa = f_ref[p, pl.ds(a, 1), :]
            term = jnp.where(r1_ref[p] < cnt, e1_ref[p], 0.0) * fa
            w = term if w is None else w + term
        parts.append(w * act[al * nk:(al + 1) * nk, :])
    wa = parts[0] if ac == 1 else jnp.concatenate(parts, axis=0)
    o_ref[...] += _dot_tn(wa.astype(BF16), v_ref[...])


def _peer_dense(h, u_tab, v_tab, r1, c0, f, e1):
    n, d = h.shape
    nk = PEER_NKEYS
    tm = _pick(n, (512, 256, 128))
    ac = 2
    ec = ac * nk
    rspec = lambda: pl.BlockSpec((PEER_HEADS, nk, tm), lambda i, c: (0, 0, i))
    return pl.pallas_call(
        functools.partial(_peer_body, ac=ac),
        grid=(n // tm, nk // ac),
        in_specs=[pl.BlockSpec((tm, d), lambda i, c: (i, 0)),
                  pl.BlockSpec((ec, d), lambda i, c: (c, 0)),
                  pl.BlockSpec((ec, d), lambda i, c: (c, 0)),
                  rspec(), rspec(), rspec(), rspec()],
        out_specs=pl.BlockSpec((tm, d), lambda i, c: (i, 0)),
        out_shape=jax.ShapeDtypeStruct((n, d), F32),
        compiler_params=_params(("parallel", "arbitrary")),
        name="peer_dense",
    )(h, u_tab, v_tab, r1, c0, f, e1)


def kernel(x_prompt, x_sample, mem_prompt, state_conv, state_ret, cache_mem_k, cache_mem_v, norm_mix, norm_mem, w_in, conv_w, conv_b, conv_ln_g, conv_ln_b, w_conv_out, ret_gn_g, w_ret_out, w_mem_k, w_mem_v, w_mem_out, w_out, norm_ffn, w_peer_q, peer_subkeys, peer_u, peer_v, norm_final):
    nb, t_len, d = x_prompt.shape
    db, dt, _ = x_sample.shape
    n_mem = mem_prompt.shape[1]
    depth = norm_mix.shape[0]
    cc = conv_w.shape[2]
    dk = state_ret.shape[3]
    dh = cache_mem_k.shape[4]
    n_p = nb * t_len
    sizes = (2 * cc, RET_HEADS * dk, RET_HEADS * dk, RET_HEADS * dk, RET_HEADS * dk, MEM_HEADS * dh, N_BRANCH * d)
    offs = [0]
    for s in sizes:
        offs.append(offs[-1] + s)
    off_q, off_m, off_gate = offs[1], offs[5], offs[6]

    x = jnp.concatenate([x_prompt.reshape(n_p, d), x_sample.reshape(db * dt, d)], axis=0)
    mem = mem_prompt.reshape(nb * n_mem, d)
    conv_p, ret_p, memk_p, memv_p, conv_s, ret_s = [], [], [], [], [], []
    for l in range(depth):
        bf = lambda a: a[l].astype(BF16)
        mh = _rmsnorm(mem, norm_mem[l], BF16)
        mk = _matmul(mh, bf(w_mem_k))
        mv = _matmul(mh, bf(w_mem_v))
        h = _rmsnorm(x, norm_mix[l], BF16)
        z = _matmul(h, bf(w_in))
        zs = z[n_p:].reshape(db, dt, z.shape[1])
        ca_p, cp = _conv_prompt(z, nb, t_len, cc, conv_w[l], conv_b[l], conv_ln_g[l], conv_ln_b[l])
        ca_s, cs = _conv_sample(zs, state_conv[l], cc, conv_w[l], conv_b[l], conv_ln_g[l], conv_ln_b[l])
        rb_p, rp = _ret_prompt(z, nb, t_len, dk, off_q, ret_gn_g[l])
        rb_s, rs = _ret_sample(zs, state_ret[l], dk, off_q, ret_gn_g[l])
        mc_p = _attn_prompt(z, mk, mv, nb, t_len, n_mem, dh, off_m)
        mc_s = _attn_sample(zs, cache_mem_k[l].reshape(db, n_mem, MEM_HEADS * dh),
                            cache_mem_v[l].reshape(db, n_mem, MEM_HEADS * dh), dh, off_m)
        cat = lambda p, s: jnp.concatenate([p, s.reshape(db * dt, s.shape[2])], axis=0)
        merged = _merge(cat(ca_p, ca_s), cat(rb_p, rb_s), cat(mc_p, mc_s),
                        bf(w_conv_out), bf(w_ret_out), bf(w_mem_out), z, off_gate)
        x = _matmul(merged, bf(w_out), residual=x)
        h2 = _rmsnorm(x, norm_ffn[l], BF16)
        q = _matmul(h2, bf(w_peer_q))
        r1, c0, f, e1 = _route(q, bf(peer_subkeys))
        pe = _peer_dense(h2, bf(peer_u), bf(peer_v), r1, c0, f, e1)
        if l + 1 < depth:
            x = x + pe
        conv_p.append(cp)
        ret_p.append(rp)
        memk_p.append(mk.reshape(nb, n_mem, MEM_HEADS, dh))
        memv_p.append(mv.reshape(nb, n_mem, MEM_HEADS, dh))
        conv_s.append(cs)
        ret_s.append(rs)
    y = _add_rmsnorm(x, pe, norm_final)
    y_prompt = y[:n_p].reshape(nb, t_len, d)
    y_sample = y[n_p:].reshape(db, dt, d)
    return (y_prompt, y_sample, jnp.stack(conv_p), jnp.stack(ret_p), jnp.stack(memk_p),
            jnp.stack(memv_p), jnp.stack(conv_s), jnp.stack(ret_s))
```

```python
import functools
import math

import jax
import jax.numpy as jnp
from jax import lax
from jax.experimental import pallas as pl
from jax.experimental.pallas import tpu as pltpu

F32 = jnp.float32
BF16 = jnp.bfloat16

EPS = 1e-6
CONV_WIDTH = 31
CONV_STATE = CONV_WIDTH - 1
RET_HEADS = 8
RET_CHUNK = 128
MEM_HEADS = 4
N_BRANCH = 3
PEER_HEADS = 8
PEER_NKEYS = 128
PEER_TOPK = 16
ROPE_BASE = 10000.0
PAST_LEN = 16384

LANES = 128
VMEM_LIMIT = 56 * 1024 * 1024


def _params(sem):
    return pltpu.CompilerParams(dimension_semantics=sem, vmem_limit_bytes=VMEM_LIMIT)


def _pick(n, prefs):
    for p in prefs:
        if n % p == 0:
            return p
    raise ValueError(f"no tile in {prefs} divides {n}")


def _dot(a, b):
    return jnp.dot(a, b, preferred_element_type=F32)


def _dot_nt(a, b):
    return lax.dot_general(a, b, (((1,), (1,)), ((), ())), preferred_element_type=F32)


def _dot_tn(a, b):
    return lax.dot_general(a, b, (((0,), (0,)), ((), ())), preferred_element_type=F32)


def _rmsnorm_rows(x, g, dtype):
    ms = jnp.mean(x * x, axis=-1, keepdims=True)
    return (x * lax.rsqrt(ms + EPS) * g).astype(dtype)


def _rmsnorm_body(x_ref, g_ref, o_ref):
    o_ref[...] = _rmsnorm_rows(x_ref[...], g_ref[...], o_ref.dtype)


def _rmsnorm(x, g, out_dtype):
    n, d = x.shape
    tm = _pick(n, (256, 128, 64, 32, 16, 8))
    return pl.pallas_call(
        _rmsnorm_body,
        grid=(n // tm,),
        in_specs=[pl.BlockSpec((tm, d), lambda i: (i, 0)),
                  pl.BlockSpec((1, d), lambda i: (0, 0))],
        out_specs=pl.BlockSpec((tm, d), lambda i: (i, 0)),
        out_shape=jax.ShapeDtypeStruct((n, d), out_dtype),
        compiler_params=_params(("parallel",)),
        name="rmsnorm",
    )(x, g.reshape(1, d))


def _rmsnorm2_body(xa_ref, xb_ref, g_ref, o_ref, *, na):
    x = jnp.where(pl.program_id(0) < na, xa_ref[...], xb_ref[...])
    o_ref[...] = _rmsnorm_rows(x, g_ref[...], o_ref.dtype)


def _rmsnorm2(xa, xb, g, out_dtype):
    (ra, d), rb = xa.shape, xb.shape[0]
    tm = _pick(math.gcd(ra, rb), (256, 128, 64, 32, 16, 8))
    na, nb_ = ra // tm, rb // tm
    return pl.pallas_call(
        functools.partial(_rmsnorm2_body, na=na),
        grid=(na + nb_,),
        in_specs=[pl.BlockSpec((tm, d), lambda i: (jnp.minimum(i, na - 1), 0)),
                  pl.BlockSpec((tm, d), lambda i: (jnp.maximum(i - na, 0), 0)),
                  pl.BlockSpec((1, d), lambda i: (0, 0))],
        out_specs=pl.BlockSpec((tm, d), lambda i: (i, 0)),
        out_shape=jax.ShapeDtypeStruct((ra + rb, d), out_dtype),
        compiler_params=_params(("parallel",)),
        name="rmsnorm2",
    )(xa, xb, g.reshape(1, d))


def _add_rmsnorm_body(x_ref, y_ref, g_ref, o_ref):
    o_ref[...] = _rmsnorm_rows(x_ref[...] + y_ref[...], g_ref[...], o_ref.dtype)


def _add_rmsnorm(x, y, g, row0, rows):
    n, d = x.shape
    tm = _pick(math.gcd(rows, row0) if row0 else rows, (256, 128, 64, 32, 16, 8))
    b0 = row0 // tm
    return pl.pallas_call(
        _add_rmsnorm_body,
        grid=(rows // tm,),
        in_specs=[pl.BlockSpec((tm, d), lambda i: (b0 + i, 0)),
                  pl.BlockSpec((tm, d), lambda i: (b0 + i, 0)),
                  pl.BlockSpec((1, d), lambda i: (0, 0))],
        out_specs=pl.BlockSpec((tm, d), lambda i: (i, 0)),
        out_shape=jax.ShapeDtypeStruct((rows, d), F32),
        compiler_params=_params(("parallel",)),
        name="add_rmsnorm",
    )(x, y, g.reshape(1, d))


def _mm_body(a_ref, w_ref, o_ref):
    o_ref[...] = _dot(a_ref[...], w_ref[...]).astype(o_ref.dtype)


def _mm_res_body(a_ref, w_ref, ra_ref, rb_ref, o_ref, *, na):
    r = jnp.where(pl.program_id(0) < na, ra_ref[...], rb_ref[...])
    o_ref[...] = r + _dot(a_ref[...], w_ref[...])


def _matmul(a, w, out_dtype=F32):
    m, k = a.shape
    n = w.shape[1]
    tm = _pick(m, (1088, 1024, 512, 256, 128))
    tn = _pick(n, (1024, 512, 256, 128))
    return pl.pallas_call(
        _mm_body,
        grid=(m // tm, n // tn),
        in_specs=[pl.BlockSpec((tm, k), lambda i, j: (i, 0)),
                  pl.BlockSpec((k, tn), lambda i, j: (0, j))],
        out_specs=pl.BlockSpec((tm, tn), lambda i, j: (i, j)),
        out_shape=jax.ShapeDtypeStruct((m, n), out_dtype),
        compiler_params=_params(("parallel", "parallel")),
        name="matmul",
    )(a, w)


def _matmul_residual(a, w, ra, rb):
    m, k = a.shape
    n = w.shape[1]
    tm = _pick(math.gcd(ra.shape[0], rb.shape[0]), (512, 256, 128))
    tn = _pick(n, (1024, 512, 256, 128))
    na = ra.shape[0] // tm
    return pl.pallas_call(
        functools.partial(_mm_res_body, na=na),
        grid=(m // tm, n // tn),
        in_specs=[pl.BlockSpec((tm, k), lambda i, j: (i, 0)),
                  pl.BlockSpec((k, tn), lambda i, j: (0, j)),
                  pl.BlockSpec((tm, tn), lambda i, j: (jnp.minimum(i, na - 1), j)),
                  pl.BlockSpec((tm, tn), lambda i, j: (jnp.maximum(i - na, 0), j))],
        out_specs=pl.BlockSpec((tm, tn), lambda i, j: (i, j)),
        out_shape=jax.ShapeDtypeStruct((m, n), F32),
        compiler_params=_params(("parallel", "parallel")),
        name="matmul_residual",
    )(a, w, ra, rb)


def _conv_tail(c, cb_ref, lg_ref, lb_ref):
    c = c + cb_ref[...]
    mu = jnp.mean(c, axis=-1, keepdims=True)
    cc = c - mu
    y = cc * lax.rsqrt(jnp.mean(cc * cc, axis=-1, keepdims=True) + EPS)
    y = y * lg_ref[...] + lb_ref[...]
    return y * jax.nn.sigmoid(y)


def _conv_prompt_body(za_ref, zb_ref, cw_ref, cb_ref, lg_ref, lb_ref, o_ref, st_ref, full_ref, *, tt):
    t = pl.program_id(1)
    halo0 = 32 - CONV_STATE

    @pl.when(t == 0)
    def _():
        full_ref[0:32, :] = jnp.zeros((32, full_ref.shape[1]), F32)

    u = za_ref[...] * jax.nn.sigmoid(zb_ref[...])
    full_ref[32:32 + tt, :] = u
    c = jnp.zeros_like(u)
    for w in range(CONV_WIDTH):
        c = c + full_ref[halo0 + w:halo0 + w + tt, :] * cw_ref[w:w + 1, :]
    o_ref[...] = _conv_tail(c, cb_ref, lg_ref, lb_ref).astype(o_ref.dtype)
    tail = full_ref[tt + halo0:tt + 32, :]
    full_ref[halo0:32, :] = tail
    st_ref[0] = tail


def _conv_prompt(z, nb, t_len, cc, conv_w, conv_b, ln_g, ln_b):
    tt = _pick(t_len, (256, 128))
    nt = t_len // tt
    vec = lambda: pl.BlockSpec((1, cc), lambda b, t: (0, 0))
    return pl.pallas_call(
        functools.partial(_conv_prompt_body, tt=tt),
        grid=(nb, nt),
        in_specs=[pl.BlockSpec((tt, cc), lambda b, t: (b * nt + t, 0)),
                  pl.BlockSpec((tt, cc), lambda b, t: (b * nt + t, 1)),
                  pl.BlockSpec((CONV_WIDTH, cc), lambda b, t: (0, 0)),
                  vec(), vec(), vec()],
        out_specs=[pl.BlockSpec((tt, cc), lambda b, t: (b * nt + t, 0)),
                   pl.BlockSpec((1, CONV_STATE, cc), lambda b, t: (b, 0, 0))],
        out_shape=[jax.ShapeDtypeStruct((nb * t_len, cc), BF16),
                   jax.ShapeDtypeStruct((nb, CONV_STATE, cc), F32)],
        scratch_shapes=[pltpu.VMEM((32 + tt, cc), F32)],
        compiler_params=_params(("parallel", "arbitrary")),
        name="conv_prompt",
    )(z, z, conv_w, conv_b.reshape(1, cc), ln_g.reshape(1, cc), ln_b.reshape(1, cc))


def _conv_sample_body(za_ref, zb_ref, st_ref, cw_ref, cb_ref, lg_ref, lb_ref, o_ref, ns_ref, full_ref, *, bb, dt):
    for i in range(bb):
        u = za_ref[i] * jax.nn.sigmoid(zb_ref[i])
        full_ref[0:CONV_STATE, :] = st_ref[i]
        full_ref[CONV_STATE:CONV_STATE + dt, :] = u
        c = jnp.zeros_like(u)
        for w in range(CONV_WIDTH):
            c = c + full_ref[w:w + dt, :] * cw_ref[w:w + 1, :]
        o_ref[i] = _conv_tail(c, cb_ref, lg_ref, lb_ref).astype(o_ref.dtype)
        ns_ref[i] = full_ref[dt:dt + CONV_STATE, :]


def _conv_sample(zs, state, cc, conv_w, conv_b, ln_g, ln_b):
    db, dt, _ = zs.shape
    bb = _pick(db, (8, 4, 2, 1))
    vec = lambda: pl.BlockSpec((1, cc), lambda b: (0, 0))
    return pl.pallas_call(
        functools.partial(_conv_sample_body, bb=bb, dt=dt),
        grid=(db // bb,),
        in_specs=[pl.BlockSpec((bb, dt, cc), lambda b: (b, 0, 0)),
                  pl.BlockSpec((bb, dt, cc), lambda b: (b, 0, 1)),
                  pl.BlockSpec((bb, CONV_STATE, cc), lambda b: (b, 0, 0)),
                  pl.BlockSpec((CONV_WIDTH, cc), lambda b: (0, 0)),
                  vec(), vec(), vec()],
        out_specs=[pl.BlockSpec((bb, dt, cc), lambda b: (b, 0, 0)),
                   pl.BlockSpec((bb, CONV_STATE, cc), lambda b: (b, 0, 0))],
        out_shape=[jax.ShapeDtypeStruct((db, dt, cc), BF16),
                   jax.ShapeDtypeStruct((db, CONV_STATE, cc), F32)],
        scratch_shapes=[pltpu.VMEM((CONV_STATE + dt + 6, cc), F32)],
        compiler_params=_params(("parallel",)),
        name="conv_sample",
    )(zs, zs, state, conv_w, conv_b.reshape(1, cc), ln_g.reshape(1, cc), ln_b.reshape(1, cc))


def _ret_log_gamma():
    return jnp.log1p(-jnp.exp2(-5.0 - jnp.arange(RET_HEADS, dtype=F32)))


def _ret_tables(c):
    lg = _ret_log_gamma()
    idx = jnp.arange(c, dtype=F32)
    diff = idx[:, None] - idx[None, :]
    decay_in = jnp.where(diff[None] >= 0.0,
                         jnp.exp(jnp.maximum(diff, 0.0)[None] * lg[:, None, None]), 0.0)
    q_dec = jnp.exp((idx + 1.0)[:, None] * lg[None, :])
    k_dec = jnp.exp((c - 1.0 - idx)[:, None] * lg[None, :])
    c_dec = jnp.exp(c * lg)
    return decay_in, q_dec.T[:, :, None], k_dec.T[:, :, None], c_dec


def _rope_tables(t_len, pos0, half):
    pos = jnp.arange(t_len, dtype=F32) + float(pos0)
    inv = ROPE_BASE ** (-jnp.arange(half, dtype=F32) / half)
    ang = pos[:, None] * inv[None, :]
    return jnp.cos(ang), jnp.sin(ang)


def _rope(x, cos, sin, half):
    x1, x2 = x[:, :half], x[:, half:]
    return jnp.concatenate([x1 * cos - x2 * sin, x1 * sin + x2 * cos], axis=-1)


def _ret_step(q, k, v, g, s, cos, sin, dec, qd, kd, cdec, gn, dk):
    half = dk // 2
    qr = _rope(q, cos, sin, half) * (dk ** -0.5)
    kr = _rope(k, cos, sin, half)
    vb = v.astype(BF16)
    att = _dot_nt(qr.astype(BF16), kr.astype(BF16)) * dec
    o = _dot(att.astype(BF16), vb) + _dot((qr * qd).astype(BF16), s.astype(BF16))
    s_new = s * cdec + _dot_tn((kr * kd).astype(BF16), vb)
    mu = jnp.mean(o, axis=-1, keepdims=True)
    oc = o - mu
    on = oc * lax.rsqrt(jnp.mean(oc * oc, axis=-1, keepdims=True) + EPS) * gn
    return on * (g * jax.nn.sigmoid(g)), s_new


def _ret_heads(cdec_ref, q_ref, k_ref, v_ref, g_ref, s_in, cos_ref, sin_ref, dec_ref, qd_ref, kd_ref, gn_ref,
               o_ref, s_out, dk):
    cos = cos_ref[...]
    sin = sin_ref[...]
    for h in range(RET_HEADS):
        sl = slice(h * dk, (h + 1) * dk)
        out, s_new = _ret_step(q_ref[:, sl], k_ref[:, sl], v_ref[:, sl], g_ref[:, sl], s_in(h), cos, sin,
                               dec_ref[h], qd_ref[h], kd_ref[h], cdec_ref[h], gn_ref[:, sl], dk)
        o_ref[:, sl] = out.astype(o_ref.dtype)
        s_out(h, s_new)


def _ret_prompt_body(cdec_ref, q_ref, k_ref, v_ref, g_ref, cos_ref, sin_ref, dec_ref, qd_ref, kd_ref, gn_ref,
                     o_ref, st_ref, s_ref, *, dk):
    c = pl.program_id(1)

    @pl.when(c == 0)
    def _():
        s_ref[...] = jnp.zeros_like(s_ref)

    def s_out(h, val):
        s_ref[h] = val

    _ret_heads(cdec_ref, q_ref, k_ref, v_ref, g_ref, lambda h: s_ref[h], cos_ref, sin_ref, dec_ref, qd_ref, kd_ref,
               gn_ref, o_ref, s_out, dk)

    @pl.when(c == pl.num_programs(1) - 1)
    def _():
        st_ref[0] = s_ref[...]


def _ret_prompt(z, nb, t_len, dk, off_q, gn_g):
    ch = math.gcd(t_len, RET_CHUNK)
    nc = t_len // ch
    nh = RET_HEADS
    w = nh * dk
    decay_in, qd, kd, cdec = _ret_tables(ch)
    cos, sin = _rope_tables(t_len, 0, dk // 2)
    oq = off_q // w
    zspec = lambda sec: pl.BlockSpec((ch, w), lambda b, c: (b * nc + c, oq + sec))
    full = lambda shp: pl.BlockSpec(shp, lambda b, c: (0,) * len(shp))
    return pl.pallas_call(
        functools.partial(_ret_prompt_body, dk=dk),
        grid=(nb, nc),
        in_specs=[pl.BlockSpec(memory_space=pltpu.SMEM),
                  zspec(0), zspec(1), zspec(2), zspec(3),
                  pl.BlockSpec((ch, dk // 2), lambda b, c: (c, 0)),
                  pl.BlockSpec((ch, dk // 2), lambda b, c: (c, 0)),
                  full((nh, ch, ch)), full((nh, ch, 1)), full((nh, ch, 1)), full((1, w))],
        out_specs=[pl.BlockSpec((ch, w), lambda b, c: (b * nc + c, 0)),
                   pl.BlockSpec((1, nh, dk, dk), lambda b, c: (b, 0, 0, 0))],
        out_shape=[jax.ShapeDtypeStruct((nb * t_len, w), BF16),
                   jax.ShapeDtypeStruct((nb, nh, dk, dk), F32)],
        scratch_shapes=[pltpu.VMEM((nh, dk, dk), F32)],
        compiler_params=_params(("parallel", "arbitrary")),
        name="ret_prompt",
    )(cdec, z, z, z, z, cos, sin, decay_in, qd, kd, gn_g.reshape(1, w))


def _ret_sample_body(cdec_ref, q_ref, k_ref, v_ref, g_ref, s_ref, cos_ref, sin_ref, dec_ref, qd_ref, kd_ref, gn_ref,
                     o_ref, st_ref, *, dk):
    def s_out(h, val):
        st_ref[0, h] = val

    _ret_heads(cdec_ref, q_ref.at[0], k_ref.at[0], v_ref.at[0], g_ref.at[0], lambda h: s_ref[0, h], cos_ref, sin_ref,
               dec_ref, qd_ref, kd_ref, gn_ref, o_ref.at[0], s_out, dk)


def _ret_sample(zs, state, dk, off_q, gn_g):
    db, dt, _ = zs.shape
    nh = RET_HEADS
    ch = math.gcd(dt, RET_CHUNK)
    assert ch == dt, "sample sequences longer than one retention chunk are not supported"
    decay_in, qd, kd, cdec = _ret_tables(ch)
    cos, sin = _rope_tables(dt, PAST_LEN, dk // 2)
    w = nh * dk
    oq = off_q // w
    zspec = lambda sec: pl.BlockSpec((1, dt, w), lambda b: (b, 0, oq + sec))
    full = lambda shp: pl.BlockSpec(shp, lambda b: (0,) * len(shp))
    return pl.pallas_call(
        functools.partial(_ret_sample_body, dk=dk),
        grid=(db,),
        in_specs=[pl.BlockSpec(memory_space=pltpu.SMEM),
                  zspec(0), zspec(1), zspec(2), zspec(3),
                  pl.BlockSpec((1, nh, dk, dk), lambda b: (b, 0, 0, 0)),
                  full((dt, dk // 2)), full((dt, dk // 2)),
                  full((nh, ch, ch)), full((nh, ch, 1)), full((nh, ch, 1)), full((1, w))],
        out_specs=[pl.BlockSpec((1, dt, w), lambda b: (b, 0, 0)),
                   pl.BlockSpec((1, nh, dk, dk), lambda b: (b, 0, 0, 0))],
        out_shape=[jax.ShapeDtypeStruct((db, dt, w), BF16),
                   jax.ShapeDtypeStruct((db, nh, dk, dk), F32)],
        compiler_params=_params(("parallel",)),
        name="ret_sample",
    )(cdec, zs, zs, zs, zs, state, cos, sin, decay_in, qd, kd, gn_g.reshape(1, w))


def _attn_heads(q, k_all, v_all, dh):
    outs = []
    for h in range(MEM_HEADS):
        sl = slice(h * dh, (h + 1) * dh)
        sc = _dot_nt(q[:, sl].astype(BF16), k_all[:, sl].astype(BF16)) * (dh ** -0.5)
        e = jnp.exp(sc - jnp.max(sc, axis=-1, keepdims=True))
        p = e / jnp.sum(e, axis=-1, keepdims=True)
        outs.append(_dot(p.astype(BF16), v_all[:, sl].astype(BF16)))
    return jnp.concatenate(outs, axis=-1)


def _attn_prompt_body(q_ref, k_ref, v_ref, o_ref, *, dh):
    o_ref[...] = _attn_heads(q_ref[...], k_ref[...], v_ref[...], dh).astype(o_ref.dtype)


def _attn_prompt(z, mk, mv, nb, t_len, n_mem, dh, off_m):
    w = MEM_HEADS * dh
    tq = _pick(t_len, (512, 256, 128))
    nt = t_len // tq
    om = off_m // w
    return pl.pallas_call(
        functools.partial(_attn_prompt_body, dh=dh),
        grid=(nb, nt),
        in_specs=[pl.BlockSpec((tq, w), lambda b, t: (b * nt + t, om)),
                  pl.BlockSpec((n_mem, w), lambda b, t: (b, 0)),
                  pl.BlockSpec((n_mem, w), lambda b, t: (b, 0))],
        out_specs=pl.BlockSpec((tq, w), lambda b, t: (b * nt + t, 0)),
        out_shape=jax.ShapeDtypeStruct((nb * t_len, w), BF16),
        compiler_params=_params(("parallel", "parallel")),
        name="attn_prompt",
    )(z, mk, mv)


def _attn_sample_body(q_ref, k_ref, v_ref, o_ref, *, dh):
    o_ref[0] = _attn_heads(q_ref[0], k_ref[0], v_ref[0], dh).astype(o_ref.dtype)


def _attn_sample(zs, ck, cv, dh, off_m):
    db, dt, _ = zs.shape
    n_mem = ck.shape[1]
    w = MEM_HEADS * dh
    om = off_m // w
    return pl.pallas_call(
        functools.partial(_attn_sample_body, dh=dh),
        grid=(db,),
        in_specs=[pl.BlockSpec((1, dt, w), lambda b: (b, 0, om)),
                  pl.BlockSpec((1, n_mem, w), lambda b: (b, 0, 0)),
                  pl.BlockSpec((1, n_mem, w), lambda b: (b, 0, 0))],
        out_specs=pl.BlockSpec((1, dt, w), lambda b: (b, 0, 0)),
        out_shape=jax.ShapeDtypeStruct((db, dt, w), BF16),
        compiler_params=_params(("parallel",)),
        name="attn_sample",
    )(zs, ck, cv)


def _merge_body(ca_ref, rb_ref, mc_ref, wc_ref, wr_ref, wm_ref, g0_ref, g1_ref, g2_ref, o_ref):
    m = jax.nn.sigmoid(g0_ref[...]) * _dot(ca_ref[...], wc_ref[...])
    m = m + jax.nn.sigmoid(g1_ref[...]) * _dot(rb_ref[...], wr_ref[...])
    m = m + jax.nn.sigmoid(g2_ref[...]) * _dot(mc_ref[...], wm_ref[...])
    o_ref[...] = m.astype(o_ref.dtype)


def _merge(ca, rb, mc, wc, wr, wm, z, off_gate):
    n = ca.shape[0]
    d = wc.shape[1]
    tm = _pick(n, (1088, 1024, 512, 256, 128))
    tn = 512
    og = off_gate // tn
    nd = d // tn
    act = lambda a: pl.BlockSpec((tm, a.shape[1]), lambda i, j: (i, 0))
    wgt = lambda a: pl.BlockSpec((a.shape[0], tn), lambda i, j: (0, j))
    gate = lambda b: pl.BlockSpec((tm, tn), lambda i, j: (i, og + b * nd + j))
    return pl.pallas_call(
        _merge_body,
        grid=(n // tm, nd),
        in_specs=[act(ca), act(rb), act(mc), wgt(wc), wgt(wr), wgt(wm), gate(0), gate(1), gate(2)],
        out_specs=pl.BlockSpec((tm, tn), lambda i, j: (i, j)),
        out_shape=jax.ShapeDtypeStruct((n, d), BF16),
        compiler_params=_params(("parallel", "parallel")),
        name="merge",
    )(ca, rb, mc, wc, wr, wm, z, z, z)


def _top16(x, iota, n):
    work = x
    rank = jnp.full(x.shape, float(PEER_TOPK), F32)
    vals, idxs = [], []
    for r in range(PEER_TOPK):
        m = jnp.max(work, axis=0, keepdims=True)
        idx = jnp.min(jnp.where(work == m, iota, float(n)), axis=0, keepdims=True)
        hit = iota == idx
        rank = jnp.where(hit, float(r), rank)
        work = jnp.where(hit, -jnp.inf, work)
        vals.append(m)
        idxs.append(idx)
    return jnp.concatenate(vals, axis=0), jnp.concatenate(idxs, axis=0), rank


_CAND_GROUPS = ((0, 0, 8), (0, 8, 8), (1, 0, 8), (2, 0, 5), (3, 0, 4), (4, 0, 3), (5, 0, 2), (6, 0, 2), (7, 0, 2))


def _route_body(q_ref, sk_ref, r1_ref, c0_ref, f_ref, e1_ref):
    nk = PEER_NKEYS
    tm = q_ref.shape[0]
    iota = lax.broadcasted_iota(jnp.int32, (nk, tm), 0).astype(F32)
    s0 = _dot_nt(sk_ref[0, 0], q_ref[:, :nk].astype(BF16))
    s1 = _dot_nt(sk_ref[0, 1], q_ref[:, nk:].astype(BF16))
    sv0, _, rank0 = _top16(s0, iota, nk)
    sv1, _, rank1 = _top16(s1, iota, nk)
    iota8 = lax.broadcasted_iota(jnp.int32, (8, tm), 0)
    groups = []
    for i, j0, cnt in _CAND_GROUPS:
        g = sv0[i:i + 1] + sv1[j0:j0 + 8]
        groups.append(g if cnt == 8 else jnp.where(iota8 < cnt, g, -jnp.inf))
    groups.append(sv0[8:16] + sv1[0:1])
    cand = jnp.concatenate(groups, axis=0)
    iota2 = lax.broadcasted_iota(jnp.int32, cand.shape, 0).astype(F32)
    tv, tp, _ = _top16(cand, iota2, cand.shape[0])
    irow = jnp.where(tp < 16.0, 0.0, jnp.where(tp < 72.0, jnp.floor(tp * 0.125) - 1.0, tp - 64.0))
    c0 = jnp.zeros((nk, tm), F32)
    for i in range(PEER_TOPK):
        cnt = jnp.sum(jnp.where(irow == float(i), 1.0, 0.0), axis=0, keepdims=True)
        c0 = jnp.where(rank0 == float(i), cnt, c0)
    z = jnp.sum(jnp.exp(tv - tv[0:1]), axis=0, keepdims=True)
    r1_ref[0] = rank1
    c0_ref[0] = c0
    f_ref[0] = jnp.exp(s0 - sv0[0:1]) / z
    e1_ref[0] = jnp.exp(s1 - sv1[0:1])


def _route(q, subkeys):
    n = q.shape[0]
    nk = PEER_NKEYS
    tm = _pick(n, (256, 128))
    spec = pl.BlockSpec((1, nk, tm), lambda i, p: (p, 0, i))
    shp = jax.ShapeDtypeStruct((PEER_HEADS, nk, n), F32)
    return pl.pallas_call(
        _route_body,
        grid=(n // tm, PEER_HEADS),
        in_specs=[pl.BlockSpec((tm, 2 * nk), lambda i, p: (i, p)),
                  pl.BlockSpec((1, 2, nk, nk), lambda i, p: (p, 0, 0, 0))],
        out_specs=[spec, spec, spec, spec],
        out_shape=[shp, shp, shp, shp],
        compiler_params=_params(("parallel", "parallel")),
        name="peer_route",
    )(q, subkeys)


def _peer_body(h_ref, u_ref, v_ref, r1_ref, c0_ref, f_ref, e1_ref, o_ref, wa_ref, *, ac):
    c = pl.program_id(1)
    nk = PEER_NKEYS
    tm = o_ref.shape[0]

    @pl.when(c == 0)
    def _():
        o_ref[...] = jnp.zeros_like(o_ref)

    act = jax.nn.gelu(_dot_nt(h_ref[...], u_ref[...]))
    for al in range(ac):
        sl = slice(al * nk, (al + 1) * nk)
        for t in range(tm // LANES):
            tok = slice(t * LANES, (t + 1) * LANES)
            w = None
            for p in range(PEER_HEADS):
                cnt = c0_ref[p, pl.ds(c * ac + al, 1), :][:, tok]
                fa = f_ref[p, pl.ds(c * ac + al, 1), :][:, tok]
                term = jnp.where(r1_ref[p, :, tok] < cnt, e1_ref[p, :, tok], 0.0) * fa
                w = term if w is None else w + term
            wa_ref[tok, sl] = (w.T * act[tok, sl]).astype(wa_ref.dtype)
    o_ref[...] += _dot(wa_ref[...], v_ref[...])


def _peer_dense(h, u_tab, v_tab, r1, c0, f, e1):
    n, d = h.shape
    nk = PEER_NKEYS
    tm = _pick(n, (512, 256, 128))
    ac = 4
    ec = ac * nk
    rspec = lambda: pl.BlockSpec((PEER_HEADS, nk, tm), lambda i, c: (0, 0, i), pipeline_mode=pl.Buffered(1))
    return pl.pallas_call(
        functools.partial(_peer_body, ac=ac),
        grid=(n // tm, nk // ac),
        in_specs=[pl.BlockSpec((tm, d), lambda i, c: (i, 0)),
                  pl.BlockSpec((ec, d), lambda i, c: (c, 0)),
                  pl.BlockSpec((ec, d), lambda i, c: (c, 0)),
                  rspec(), rspec(), rspec(), rspec()],
        out_specs=pl.BlockSpec((tm, d), lambda i, c: (i, 0)),
        out_shape=jax.ShapeDtypeStruct((n, d), F32),
        scratch_shapes=[pltpu.VMEM((tm, ec), BF16)],
        compiler_params=_params(("parallel", "arbitrary")),
        name="peer_dense",
    )(h, u_tab, v_tab, r1, c0, f, e1)


def kernel(x_prompt, x_sample, mem_prompt, state_conv, state_ret, cache_mem_k, cache_mem_v, norm_mix, norm_mem, w_in, conv_w, conv_b, conv_ln_g, conv_ln_b, w_conv_out, ret_gn_g, w_ret_out, w_mem_k, w_mem_v, w_mem_out, w_out, norm_ffn, w_peer_q, peer_subkeys, peer_u, peer_v, norm_final):
    nb, t_len, d = x_prompt.shape
    db, dt, _ = x_sample.shape
    n_mem = mem_prompt.shape[1]
    depth = norm_mix.shape[0]
    cc = conv_w.shape[2]
    dk = state_ret.shape[3]
    dh = cache_mem_k.shape[4]
    n_p = nb * t_len
    sizes = (2 * cc, RET_HEADS * dk, RET_HEADS * dk, RET_HEADS * dk, RET_HEADS * dk, MEM_HEADS * dh, N_BRANCH * d)
    offs = [0]
    for s in sizes:
        offs.append(offs[-1] + s)
    off_q, off_m, off_gate = offs[1], offs[5], offs[6]

    assert depth == 1, "a single layer is supported"
    l = 0
    bf = lambda a: a[l].astype(BF16)
    xp = x_prompt.reshape(n_p, d)
    xs = x_sample.reshape(db * dt, d)
    mh = _rmsnorm(mem_prompt.reshape(nb * n_mem, d), norm_mem[l], BF16)
    mk = _matmul(mh, bf(w_mem_k))
    mv = _matmul(mh, bf(w_mem_v))
    h = _rmsnorm2(xp, xs, norm_mix[l], BF16)
    z = _matmul(h, bf(w_in))
    zs = z[n_p:].reshape(db, dt, z.shape[1])
    ca_p, cp = _conv_prompt(z, nb, t_len, cc, conv_w[l], conv_b[l], conv_ln_g[l], conv_ln_b[l])
    ca_s, cs = _conv_sample(zs, state_conv[l], cc, conv_w[l], conv_b[l], conv_ln_g[l], conv_ln_b[l])
    rb_p, rp = _ret_prompt(z, nb, t_len, dk, off_q, ret_gn_g[l])
    rb_s, rs = _ret_sample(zs, state_ret[l], dk, off_q, ret_gn_g[l])
    mc_p = _attn_prompt(z, mk, mv, nb, t_len, n_mem, dh, off_m)
    mc_s = _attn_sample(zs, cache_mem_k[l].reshape(db, n_mem, MEM_HEADS * dh),
                        cache_mem_v[l].reshape(db, n_mem, MEM_HEADS * dh), dh, off_m)
    cat = lambda p, s: jnp.concatenate([p, s.reshape(db * dt, s.shape[2])], axis=0)
    merged = _merge(cat(ca_p, ca_s), cat(rb_p, rb_s), cat(mc_p, mc_s),
                    bf(w_conv_out), bf(w_ret_out), bf(w_mem_out), z, off_gate)
    x = _matmul_residual(merged, bf(w_out), xp, xs)
    h2 = _rmsnorm(x, norm_ffn[l], BF16)
    q = _matmul(h2, bf(w_peer_q))
    r1, c0, f, e1 = _route(q, bf(peer_subkeys))
    pe = _peer_dense(h2, bf(peer_u), bf(peer_v), r1, c0, f, e1)
    y_prompt = _add_rmsnorm(x, pe, norm_final, 0, n_p).reshape(nb, t_len, d)
    y_sample = _add_rmsnorm(x, pe, norm_final, n_p, db * dt).reshape(db, dt, d)
    return (y_prompt, y_sample, cp[None], rp[None], mk.reshape(1, nb, n_mem, MEM_HEADS, dh),
            mv.reshape(1, nb, n_mem, MEM_HEADS, dh), cs[None], rs[None])
```
